```python
import math
import jax, jax.numpy as jnp
from jax import lax
import numpy as np

D_MODEL = 4096
BATCH = 2
SEQ = 4096
DEPTH = 2

N_META = 16
MIX_WIDTH = D_MODEL
GROUP_WIDTH = MIX_WIDTH // 2
SSD_HEAD_DIM = 64
SSD_HEADS = GROUP_WIDTH // SSD_HEAD_DIM
SSD_GROUPS = 8
SSD_HEADS_PER_GROUP = SSD_HEADS // SSD_GROUPS
SSD_STATE = 128
SSD_CONV = 4
SSD_CHUNK = 128
SSD_CONV_DIM = GROUP_WIDTH + 2 * SSD_GROUPS * SSD_STATE
CONF_CH = GROUP_WIDTH
CONF_WIDTH = 31
HGRN_HEAD_DIM = 128
HGRN_HEADS = GROUP_WIDTH // HGRN_HEAD_DIM
HGRN_CHUNK = 64
SB_HEAD_DIM = 128
SB_HEADS = GROUP_WIDTH // SB_HEAD_DIM
SB_BLOCK = 128
D_FF = 4 * D_MODEL
N_EVEN = (DEPTH + 1) // 2
N_ODD = DEPTH // 2
EVEN_IN = GROUP_WIDTH + SSD_CONV_DIM + SSD_HEADS + 2 * CONF_CH
ODD_IN = 4 * GROUP_WIDTH + 3 * GROUP_WIDTH
EPS = 1e-6
F32 = jnp.float32

kernel_name = "hybrid_ssd_conformer_hgrn2_stickbreak"


def rmsnorm(x, w):
    xf = x.astype(F32)
    y = xf * lax.rsqrt(jnp.mean(xf * xf, axis=-1, keepdims=True) + EPS)
    return (y * w.astype(F32)).astype(x.dtype)


def group_rmsnorm(x, w, n_groups):
    shp = x.shape
    xg = x.astype(F32).reshape(shp[:-1] + (n_groups, shp[-1] // n_groups))
    xg = xg * lax.rsqrt(jnp.mean(xg * xg, axis=-1, keepdims=True) + EPS)
    return xg.reshape(shp) * w.astype(F32)


def layernorm(x, w, b):
    xf = x.astype(F32)
    mu = jnp.mean(xf, axis=-1, keepdims=True)
    xc = xf - mu
    y = xc * lax.rsqrt(jnp.mean(xc * xc, axis=-1, keepdims=True) + EPS)
    return y * w.astype(F32) + b.astype(F32)


def causal_dwconv(u, w, b):
    k_w = w.shape[0]
    out = lax.conv_general_dilated(
        u, w[:, None, :].astype(u.dtype), window_strides=(1,),
        padding=[(k_w - 1, 0)], dimension_numbers=("NWC", "WIO", "NWC"),
        feature_group_count=u.shape[-1])
    return out + b.astype(u.dtype)


def left_pad(t, n):
    pads = [(0, 0)] * t.ndim
    pads[1] = (n, 0)
    return jnp.pad(t, pads)


def ssd_mixer(z, xbc, dt_raw, conv_w, conv_b, dt_bias, a_log, d_skip, norm_w):
    bsz, seq_len, _ = z.shape
    G, R, P, N, C = SSD_GROUPS, SSD_HEADS_PER_GROUP, SSD_HEAD_DIM, SSD_STATE, SSD_CHUNK
    xbc = jax.nn.silu(causal_dwconv(xbc, conv_w, conv_b))
    xs, bm, cm = jnp.split(xbc, [GROUP_WIDTH, GROUP_WIDTH + G * N], axis=-1)
    dt = jax.nn.softplus(dt_raw.astype(F32) + dt_bias.astype(F32))
    a_head = -jnp.exp(a_log.astype(F32)).reshape(G, R)
    pad = C - N_META
    xs, bm, cm, dt = (left_pad(t, pad) for t in (xs, bm, cm, dt))
    nc = (seq_len + pad) // C
    x6 = xs.reshape(bsz, nc, C, G, R, P).astype(F32)
    bm = bm.reshape(bsz, nc, C, G, N).astype(F32)
    cm = cm.reshape(bsz, nc, C, G, N).astype(F32)
    dt = dt.reshape(bsz, nc, C, G, R)
    a_cs = jnp.cumsum(dt * a_head, axis=2)
    xdt = x6 * dt[..., None]
    tri = jnp.tril(jnp.ones((C, C), bool))
    seg = a_cs[:, :, :, None] - a_cs[:, :, None, :]
    lmat = jnp.exp(jnp.where(tri[:, :, None, None], seg, -jnp.inf))
    cb = jnp.einsum("bctgn,bcsgn->bctsg", cm, bm)
    y_diag = jnp.einsum("bctsgr,bcsgrp->bctgrp", cb[..., None] * lmat, xdt)
    decay_states = jnp.exp(a_cs[:, :, -1:] - a_cs)
    states = jnp.einsum("bclgn,bclgrp->bcgrpn", bm, xdt * decay_states[..., None])
    chunk_decay = jnp.exp(a_cs[:, :, -1])

    def step(s, inp):
        st, dec = inp
        return s * dec[..., None, None] + st, s

    s0 = jnp.zeros((bsz, G, R, P, N), F32)
    _, prev = lax.scan(step, s0, (jnp.moveaxis(states, 1, 0), jnp.moveaxis(chunk_decay, 1, 0)))
    prev = jnp.moveaxis(prev, 0, 1)
    y_off = jnp.einsum("bclgn,bcgrpn->bclgrp", cm, prev) * jnp.exp(a_cs)[..., None]
    y = y_diag + y_off + d_skip.astype(F32).reshape(G, R, 1) * x6
    y = y.reshape(bsz, nc * C, GROUP_WIDTH)[:, pad:]
    y = y * jax.nn.silu(z.astype(F32))
    return group_rmsnorm(y, norm_w, SSD_GROUPS).astype(z.dtype)


def conformer_conv(u, conv_w, conv_b, ln_w, ln_b):
    a, gate = jnp.split(u, 2, axis=-1)
    hcv = a * jax.nn.sigmoid(gate)
    hcv = causal_dwconv(hcv, conv_w, conv_b)
    return jax.nn.silu(layernorm(hcv, ln_w, ln_b)).astype(u.dtype)


def hgrn_lower_bounds(logits):
    p = jax.nn.softmax(logits.astype(F32), axis=0)
    return jnp.cumsum(p, axis=0) - p[0]


def hgrn2_mixer(q, f_raw, i_in, g, lb, norm_w):
    bsz, seq_len, _ = q.shape
    H, Dh, C = HGRN_HEADS, HGRN_HEAD_DIM, HGRN_CHUNK
    qf = jax.nn.silu(q.astype(F32))
    f = lb + (1.0 - lb) * jax.nn.sigmoid(f_raw.astype(F32))
    log_f = jnp.log(f)
    k = 1.0 - f
    v = i_in.astype(F32)
    pad = C - N_META
    nc = (seq_len + pad) // C

    def to_chunks(t):
        return left_pad(t, pad).reshape(bsz, nc, C, H, Dh).transpose(1, 0, 3, 2, 4)

    tri = jnp.tril(jnp.ones((C, C), bool))

    def step(s, inp):
        qc, kc, vc, gc = inp
        gcs = jnp.cumsum(gc, axis=2)
        diff = gcs[:, :, :, None, :] - gcs[:, :, None, :, :]
        decay = jnp.exp(jnp.where(tri[:, :, None], diff, -jnp.inf))
        att = jnp.einsum("bhtd,bhsd,bhtsd->bhts", qc, kc, decay)
        o = jnp.einsum("bhts,bhsv->bhtv", att, vc) + jnp.einsum("bhtd,bhdv->bhtv", qc * jnp.exp(gcs), s)
        glast = gcs[:, :, -1:, :]
        s = jnp.exp(glast[:, :, 0, :, None]) * s + jnp.einsum("bhsd,bhsv->bhdv", kc * jnp.exp(glast - gcs), vc)
        return s, o

    s0 = jnp.zeros((bsz, H, Dh, Dh), F32)
    _, o = lax.scan(step, s0, (to_chunks(qf), to_chunks(k), to_chunks(v), to_chunks(log_f)))
    o = o.transpose(1, 0, 3, 2, 4).reshape(bsz, nc * C, H * Dh)[:, pad:]
    o = group_rmsnorm(o, norm_w, H) * jax.nn.silu(g.astype(F32))
    return o.astype(g.dtype)


def stick_breaking_attention(q, k, v):
    bsz, seq_len, _ = q.shape
    H, Dh, Bq = SB_HEADS, SB_HEAD_DIM, SB_BLOCK
    pad = Bq - N_META
    lp = seq_len + pad

    def heads(t):
        return left_pad(t, pad).reshape(bsz, lp, H, Dh).transpose(0, 2, 1, 3)

    qh, kh, vh = heads(q), heads(k), heads(v)
    scale = Dh ** -0.5
    outs = []
    for blk in range(lp // Bq):
        end = (blk + 1) * Bq
        qb, kb, vb = qh[:, :, blk * Bq:end], kh[:, :, :end], vh[:, :, :end]
        tpos = blk * Bq + jnp.arange(Bq)
        spos = jnp.arange(end)
        keep = (spos[None, :] < tpos[:, None]) & (spos >= pad)[None, :]
        logits = jnp.einsum("bhqd,bhkd->bhqk", qb, kb).astype(F32) * scale
        log_beta = jax.nn.log_sigmoid(logits)
        log_rest = jnp.where(keep, jax.nn.log_sigmoid(-logits), 0.0)
        rev = lax.cumsum(log_rest, axis=3, reverse=True)
        after = jnp.concatenate([rev[..., 1:], jnp.zeros_like(rev[..., :1])], axis=-1)
        weights = jnp.where(keep, jnp.exp(log_beta + after), 0.0)
        outs.append(jnp.einsum("bhqk,bhkd->bhqd", weights, vb.astype(F32)))
    o = jnp.concatenate(outs, axis=2)[:, :, pad:]
    return o.transpose(0, 2, 1, 3).reshape(bsz, seq_len, H * Dh).astype(q.dtype)


def squared_relu_mlp(u, w1, w2):
    hid = jnp.square(jax.nn.relu(u @ w1))
    return hid @ w2


def setup_inputs(seed: int = 0) -> dict:
    key = jax.random.key(seed)
    ks = jax.random.split(key, 24)

    def nrm(k, shape, scale):
        return jax.random.normal(k, shape, F32) * scale

    dt0 = jnp.exp(jax.random.uniform(ks[8], (N_EVEN, SSD_HEADS), F32,
                                     minval=math.log(1e-3), maxval=math.log(1e-1)))
    return {
        "x": nrm(ks[0], (BATCH, SEQ, D_MODEL), 1.0),
        "meta_tokens": nrm(ks[1], (N_META, D_MODEL), 1.0),
        "mix_norm_w": 1.0 + nrm(ks[2], (DEPTH, D_MODEL), 0.02),
        "mlp_norm_w": 1.0 + nrm(ks[3], (DEPTH, D_MODEL), 0.02),
        "final_norm_w": 1.0 + nrm(ks[4], (D_MODEL,), 0.02),
        "w_in_even": nrm(ks[5], (N_EVEN, D_MODEL, EVEN_IN), D_MODEL ** -0.5),
        "ssd_conv_w": nrm(ks[6], (N_EVEN, SSD_CONV, SSD_CONV_DIM), SSD_CONV ** -0.5),
        "ssd_conv_b": nrm(ks[7], (N_EVEN, SSD_CONV_DIM), 0.02),
        "ssd_dt_bias": dt0 + jnp.log(-jnp.expm1(-dt0)),
        "ssd_a_log": jnp.log(jax.random.uniform(ks[9], (N_EVEN, SSD_HEADS), F32, minval=1.0, maxval=16.0)),
        "ssd_d": 1.0 + nrm(ks[10], (N_EVEN, SSD_HEADS), 0.1),
        "ssd_norm_w": 1.0 + nrm(ks[11], (N_EVEN, GROUP_WIDTH), 0.02),
        "conf_conv_w": nrm(ks[12], (N_EVEN, CONF_WIDTH, CONF_CH), CONF_WIDTH ** -0.5),
        "conf_conv_b": nrm(ks[13], (N_EVEN, CONF_CH), 0.02),
        "conf_ln_w": 1.0 + nrm(ks[14], (N_EVEN, CONF_CH), 0.02),
        "conf_ln_b": nrm(ks[15], (N_EVEN, CONF_CH), 0.02),
        "w_out_even": nrm(ks[16], (N_EVEN, 2 * GROUP_WIDTH, D_MODEL), (2 * GROUP_WIDTH) ** -0.5),
        "w_in_odd": nrm(ks[17], (N_ODD, D_MODEL, ODD_IN), D_MODEL ** -0.5),
        "hgrn_lb_logits": nrm(ks[18], (DEPTH, GROUP_WIDTH), 0.5),
        "hgrn_norm_w": 1.0 + nrm(ks[19], (N_ODD, GROUP_WIDTH), 0.02),
        "w_out_odd": nrm(ks[20], (N_ODD, 2 * GROUP_WIDTH, D_MODEL), (2 * GROUP_WIDTH) ** -0.5),
        "mlp_w1": nrm(ks[21], (DEPTH, D_MODEL, D_FF), D_MODEL ** -0.5),
        "mlp_w2": nrm(ks[22], (DEPTH, D_FF, D_MODEL), D_FF ** -0.5),
    }


def reference(x, meta_tokens, mix_norm_w, mlp_norm_w, final_norm_w, w_in_even, ssd_conv_w,
              ssd_conv_b, ssd_dt_bias, ssd_a_log, ssd_d, ssd_norm_w, conf_conv_w, conf_conv_b,
              conf_ln_w, conf_ln_b, w_out_even, w_in_odd, hgrn_lb_logits, hgrn_norm_w,
              w_out_odd, mlp_w1, mlp_w2):
    bsz = x.shape[0]
    meta = jnp.broadcast_to(meta_tokens[None].astype(x.dtype), (bsz, N_META, D_MODEL))
    h = jnp.concatenate([meta, x], axis=1)
    lower_bounds = hgrn_lower_bounds(hgrn_lb_logits)
    for layer in range(DEPTH):
        j = layer // 2
        u = rmsnorm(h, mix_norm_w[layer])
        if layer % 2 == 0:
            proj = u @ w_in_even[j]
            z, xbc, dt_raw, conf_in = jnp.split(
                proj, [GROUP_WIDTH, GROUP_WIDTH + SSD_CONV_DIM, GROUP_WIDTH + SSD_CONV_DIM + SSD_HEADS], axis=-1)
            y_ssd = ssd_mixer(z, xbc, dt_raw, ssd_conv_w[j], ssd_conv_b[j], ssd_dt_bias[j],
                              ssd_a_log[j], ssd_d[j], ssd_norm_w[j])
            y_conf = conformer_conv(conf_in, conf_conv_w[j], conf_conv_b[j], conf_ln_w[j], conf_ln_b[j])
            mixed = jnp.concatenate([y_ssd, y_conf], axis=-1) @ w_out_even[j]
        else:
            proj = u @ w_in_odd[j]
            hq, hf, hi, hg, sq, sk, sv = jnp.split(proj, 7, axis=-1)
            y_hgrn = hgrn2_mixer(hq, hf, hi, hg, lower_bounds[layer], hgrn_norm_w[j])
            y_sb = stick_breaking_attention(sq, sk, sv)
            mixed = jnp.concatenate([y_hgrn, y_sb], axis=-1) @ w_out_odd[j]
        h = h + mixed.astype(h.dtype)
        u = rmsnorm(h, mlp_norm_w[layer])
        h = h + squared_relu_mlp(u, mlp_w1[layer], mlp_w2[layer]).astype(h.dtype)
    return rmsnorm(h[:, N_META:], final_norm_w)
```

```python
import functools

import jax
import jax.numpy as jnp
from jax import lax
from jax.experimental import pallas as pl
from jax.experimental.pallas import tpu as pltpu

F32 = jnp.float32
BF16 = jnp.bfloat16

D_MODEL = 4096
N_META = 16
GROUP_WIDTH = 2048
CHUNK = 128
PAD = CHUNK - N_META
SSD_HEAD_DIM = 64
SSD_HEADS = 32
SSD_GROUPS = 8
SSD_STATE = 128
SSD_CONV = 4
SSD_GROUP_CH = GROUP_WIDTH // SSD_GROUPS
SSD_CONV_DIM = GROUP_WIDTH + 2 * SSD_GROUPS * SSD_STATE
CONF_WIDTH = 31
CONF_HALO = 32
HGRN_HEADS = 16
HGRN_CHUNK = 64
HGRN_SUB = 16
SB_HEADS = 16
HEAD_DIM = 128
D_FF = 4 * D_MODEL
EPS = 1e-6
LANES = 128
VMEM_LIMIT = 56 * 1024 * 1024


def _cparams(sem):
    return pltpu.CompilerParams(dimension_semantics=sem, vmem_limit_bytes=VMEM_LIMIT)


def _silu(x):
    return x * jax.nn.sigmoid(x)


def _split3(x):
    hi = x.astype(BF16)
    r1 = x - hi.astype(F32)
    mid = r1.astype(BF16)
    lo = (r1 - mid.astype(F32)).astype(BF16)
    return hi, mid, lo


def _dot(a, b):
    return jnp.dot(a, b, preferred_element_type=F32)


def _dot_nt(a, b):
    return lax.dot_general(a, b, (((1,), (1,)), ((), ())), preferred_element_type=F32)


def _mm_kernel(a_ref, b_ref, *rest, nk, relu2, has_res):
    if has_res:
        r_ref, rest = rest[0], rest[1:]
    o_ref = rest[0]

    def finish(acc):
        if relu2:
            acc = jnp.square(jnp.maximum(acc, 0.0))
        if has_res:
            acc = acc + r_ref[...]
        o_ref[...] = acc.astype(o_ref.dtype)

    if nk == 1:
        finish(_dot(a_ref[...], b_ref[...]))
        return
    acc_ref = rest[1]
    k = pl.program_id(2)

    @pl.when(k == 0)
    def _():
        acc_ref[...] = jnp.zeros_like(acc_ref)

    acc_ref[...] += _dot(a_ref[...], b_ref[...])

    @pl.when(k == nk - 1)
    def _():
        finish(acc_ref[...])


def _matmul(a, b, *, tm, tn, tk, out_dtype, relu2=False, residual=None, name):
    m, kdim = a.shape
    n = b.shape[1]
    nk = kdim // tk
    assert m % tm == 0 and n % tn == 0 and kdim % tk == 0
    in_specs = [pl.BlockSpec((tm, tk), lambda i, j, k: (i, k)),
                pl.BlockSpec((tk, tn), lambda i, j, k: (k, j))]
    args = [a, b]
    if residual is not None:
        in_specs.append(pl.BlockSpec((tm, tn), lambda i, j, k: (i, j)))
        args.append(residual)
    scratch = [pltpu.VMEM((tm, tn), F32)] if nk > 1 else []
    return pl.pallas_call(
        functools.partial(_mm_kernel, nk=nk, relu2=relu2, has_res=residual is not None),
        grid=(m // tm, n // tn, nk),
        in_specs=in_specs,
        out_specs=pl.BlockSpec((tm, tn), lambda i, j, k: (i, j)),
        out_shape=jax.ShapeDtypeStruct((m, n), out_dtype),
        scratch_shapes=scratch,
        compiler_params=_cparams(("parallel", "parallel", "arbitrary")),
        name=name,
    )(*args)


def _rmsnorm_kernel(x_ref, w_ref, o_ref):
    x = x_ref[...]
    y = x * lax.rsqrt(jnp.mean(x * x, axis=-1, keepdims=True) + EPS)
    o_ref[...] = (y * w_ref[...]).astype(o_ref.dtype)


def _rmsnorm(x, w, *, tm, out_dtype, name, row_block_offset=0, out_rows=None, blocks_per_batch=None):
    m, d = x.shape
    if blocks_per_batch is None:
        nblk = m // tm
        in_map = lambda i: (i, 0)
        out_rows = m
    else:
        nblk = out_rows // tm
        in_per_batch = blocks_per_batch + row_block_offset
        in_map = lambda i: ((i // blocks_per_batch) * in_per_batch + row_block_offset + i % blocks_per_batch, 0)
    return pl.pallas_call(
        _rmsnorm_kernel,
        grid=(nblk,),
        in_specs=[pl.BlockSpec((tm, d), in_map), pl.BlockSpec((1, d), lambda i: (0, 0))],
        out_specs=pl.BlockSpec((tm, d), lambda i: (i, 0)),
        out_shape=jax.ShapeDtypeStruct((out_rows, d), out_dtype),
        compiler_params=_cparams(("parallel",)),
        name=name,
    )(x, w.reshape(1, d))


def _ssd_kernel(xbc_ref, z_ref, dt_ref, cw_ref, cb_ref, dtb_ref, ah_ref, e3_ref, dx_ref, nw_ref,
                o_ref, ext_ref, state_ref, y_ref):
    c = pl.program_id(1)
    C = CHUNK

    @pl.when(c == 0)
    def _():
        ext_ref[0:8, :] = jnp.zeros((8, SSD_CONV_DIM), F32)
        state_ref[...] = jnp.zeros_like(state_ref)

    ext_ref[8:8 + C, :] = xbc_ref[...]
    acc = cb_ref[...] + cw_ref[0:1, :] * ext_ref[5:5 + C, :]
    for k in range(1, SSD_CONV):
        acc = acc + cw_ref[k:k + 1, :] * ext_ref[5 + k:5 + k + C, :]
    ext_ref[0:8, :] = ext_ref[C:C + 8, :]
    row = c * C + lax.broadcasted_iota(jnp.int32, (C, 1), 0)
    valid = row >= PAD
    act = jnp.where(valid, _silu(acc), 0.0)

    dtv = dt_ref[...] + dtb_ref[...]
    dt = jnp.maximum(dtv, 0.0) + jnp.log(1.0 + jnp.exp(-jnp.abs(dtv)))
    dt = jnp.where(valid, dt, 0.0)
    a = dt * ah_ref[...]

    ti = lax.broadcasted_iota(jnp.int32, (C, C), 0)
    si = lax.broadcasted_iota(jnp.int32, (C, C), 1)
    tril = si <= ti
    tri_b = tril.astype(BF16)
    a_cs = sum(_dot(tri_b, p) for p in _split3(a))
    stacked = jnp.concatenate([dt, a_cs], axis=0)
    exp3 = _dot(jnp.concatenate(_split3(stacked), axis=1), e3_ref[...])
    dtx = exp3[:C]
    acsx = exp3[C:]
    a_last = acsx[C - 1:C, :]
    e_acs = jnp.exp(acsx)
    dec_st = jnp.exp(a_last - acsx)
    chunk_dec = jnp.exp(a_last)

    xs = act[:, :GROUP_WIDTH]
    xdt = xs * dtx
    xdec = (xdt * dec_st).astype(BF16)
    xdt_b = xdt.astype(BF16)
    a_cs_t = a_cs.T

    for g in range(SSD_GROUPS):
        cols = slice(g * SSD_GROUP_CH, (g + 1) * SSD_GROUP_CH)
        b_f = act[:, GROUP_WIDTH + g * SSD_STATE:GROUP_WIDTH + (g + 1) * SSD_STATE]
        c_f = act[:, GROUP_WIDTH + SSD_GROUPS * SSD_STATE + g * SSD_STATE:
                  GROUP_WIDTH + SSD_GROUPS * SSD_STATE + (g + 1) * SSD_STATE]
        b_b = b_f.astype(BF16)
        c_b = c_f.astype(BF16)
        cb = _dot_nt(c_b, b_b)
        s_g = state_ref[g]
        y_off = _dot(c_b, s_g.astype(BF16)) * e_acs[:, cols]
        for r in range(SSD_HEADS // SSD_GROUPS):
            h = g * (SSD_HEADS // SSD_GROUPS) + r
            seg = a_cs[:, h:h + 1] - a_cs_t[h:h + 1, :]
            lmat = jnp.exp(jnp.where(tril, seg, -jnp.inf))
            hc = slice(h * SSD_HEAD_DIM, (h + 1) * SSD_HEAD_DIM)
            y_ref[:, hc] = _dot((cb * lmat).astype(BF16), xdt_b[:, hc])
        state_ref[g] = s_g * chunk_dec[:, cols] + _dot(b_f.T.astype(BF16), xdec[:, cols])
        y = y_ref[:, cols] + y_off + dx_ref[:, cols] * xs[:, cols]
        y = y * _silu(z_ref[:, cols])
        y = y * lax.rsqrt(jnp.mean(y * y, axis=-1, keepdims=True) + EPS)
        o_ref[:, cols] = (y * nw_ref[:, cols]).astype(o_ref.dtype)


def _ssd(proj, dt_raw, conv_w, conv_b, dt_bias, a_log, d_skip, norm_w, *, bsz, nchunks):
    lane_pad = LANES - SSD_HEADS
    dtb = jnp.pad(dt_bias, (0, lane_pad)).reshape(1, LANES)
    a_head = jnp.pad(-jnp.exp(a_log), (0, lane_pad)).reshape(1, LANES)
    expand = (jnp.arange(LANES)[:, None] == (jnp.arange(GROUP_WIDTH) // SSD_HEAD_DIM)[None, :]).astype(BF16)
    e3 = jnp.concatenate([expand] * 3, axis=0)
    dx = jnp.repeat(d_skip, SSD_HEAD_DIM).reshape(1, GROUP_WIDTH)
    nb = nchunks
    full = lambda shape: pl.BlockSpec(shape, lambda b, c: (0,) * len(shape))
    return pl.pallas_call(
        _ssd_kernel,
        grid=(bsz, nchunks),
        in_specs=[
            pl.BlockSpec((CHUNK, SSD_CONV_DIM), lambda b, c: (b * nb + c, 0)),
            pl.BlockSpec((CHUNK, GROUP_WIDTH), lambda b, c: (b * nb + c, SSD_CONV_DIM // GROUP_WIDTH)),
            pl.BlockSpec((CHUNK, LANES), lambda b, c: (b * nb + c, 0)),
            full((SSD_CONV, SSD_CONV_DIM)), full((1, SSD_CONV_DIM)), full((1, LANES)), full((1, LANES)),
            full((3 * LANES, GROUP_WIDTH)), full((1, GROUP_WIDTH)), full((1, GROUP_WIDTH)),
        ],
        out_specs=pl.BlockSpec((CHUNK, GROUP_WIDTH), lambda b, c: (b * nb + c, 0)),
        out_shape=jax.ShapeDtypeStruct((bsz * nchunks * CHUNK, GROUP_WIDTH), BF16),
        scratch_shapes=[pltpu.VMEM((CHUNK + 8, SSD_CONV_DIM), F32),
                        pltpu.VMEM((SSD_GROUPS, SSD_STATE, SSD_GROUP_CH), F32),
                        pltpu.VMEM((CHUNK, GROUP_WIDTH), F32)],
        compiler_params=_cparams(("arbitrary", "arbitrary")),
        name="ssd",
    )(proj, proj, dt_raw, conv_w, conv_b.reshape(1, -1), dtb, a_head, e3, dx, norm_w.reshape(1, -1))


def _conf_kernel(a_ref, g_ref, cw_ref, cb_ref, lnw_ref, lnb_ref, o_ref, ext_ref, conv_ref):
    i = pl.program_id(1)
    R = CHUNK
    H = CONF_HALO

    @pl.when(i == 0)
    def _():
        ext_ref[0:H, :] = jnp.zeros((H, GROUP_WIDTH), F32)

    row = i * R + lax.broadcasted_iota(jnp.int32, (R, 1), 0)
    valid = row >= PAD
    ext_ref[H:H + R, :] = jnp.where(valid, a_ref[...] * jax.nn.sigmoid(g_ref[...]), 0.0)
    first = H - (CONF_WIDTH - 1)
    for ct in range(GROUP_WIDTH // LANES):
        cols = slice(ct * LANES, (ct + 1) * LANES)
        acc = cb_ref[:, cols] + cw_ref[0:1, cols] * ext_ref[first:first + R, cols]
        for k in range(1, CONF_WIDTH):
            acc = acc + cw_ref[k:k + 1, cols] * ext_ref[first + k:first + k + R, cols]
        conv_ref[:, cols] = acc
    ext_ref[0:H, :] = ext_ref[R:R + H, :]
    y = conv_ref[...]
    mu = jnp.mean(y, axis=-1, keepdims=True)
    yc = y - mu
    yn = yc * lax.rsqrt(jnp.mean(yc * yc, axis=-1, keepdims=True) + EPS)
    yn = yn * lnw_ref[...] + lnb_ref[...]
    o_ref[...] = jnp.where(valid, _silu(yn), 0.0).astype(o_ref.dtype)


def _conformer(proj, conv_w, conv_b, ln_w, ln_b, *, bsz, nchunks):
    nb = nchunks
    a_blk = SSD_CONV_DIM // GROUP_WIDTH + 1
    full = lambda shape: pl.BlockSpec(shape, lambda b, c: (0,) * len(shape))
    return pl.pallas_call(
        _conf_kernel,
        grid=(bsz, nchunks),
        in_specs=[
            pl.BlockSpec((CHUNK, GROUP_WIDTH), lambda b, c: (b * nb + c, a_blk)),
            pl.BlockSpec((CHUNK, GROUP_WIDTH), lambda b, c: (b * nb + c, a_blk + 1)),
            full((CONF_WIDTH, GROUP_WIDTH)), full((1, GROUP_WIDTH)), full((1, GROUP_WIDTH)), full((1, GROUP_WIDTH)),
        ],
        out_specs=pl.BlockSpec((CHUNK, GROUP_WIDTH), lambda b, c: (b * nb + c, 0)),
        out_shape=jax.ShapeDtypeStruct((bsz * nchunks * CHUNK, GROUP_WIDTH), BF16),
        scratch_shapes=[pltpu.VMEM((CHUNK + CONF_HALO, GROUP_WIDTH), F32),
                        pltpu.VMEM((CHUNK, GROUP_WIDTH), F32)],
        compiler_params=_cparams(("arbitrary", "arbitrary")),
        name="conformer",
    )(proj, proj, conv_w, conv_b.reshape(1, -1), ln_w.reshape(1, -1), ln_b.reshape(1, -1))


def _hgrn_kernel(q_ref, f_ref, i_ref, g_ref, lbl_ref, nw_ref, o_ref, st_ref, kp_ref, gp_ref, vp_ref,
                 *, nchunks, layer):
    C = HGRN_CHUNK
    SUB = HGRN_SUB
    nsub = C // SUB
    logits = lbl_ref[...]
    ex = jnp.exp(logits - jnp.max(logits, axis=0, keepdims=True))
    p = ex / jnp.sum(ex, axis=0, keepdims=True)
    lb = jnp.sum(p[:layer + 1], axis=0, keepdims=True) - p[0:1]

    st_ref[...] = jnp.zeros_like(st_ref)
    zpad = jnp.zeros((SUB, HEAD_DIM), F32)
    kp_ref[0:SUB, :] = zpad
    gp_ref[0:SUB, :] = zpad
    vp_ref[0:SUB, :] = zpad

    ti = lax.broadcasted_iota(jnp.int32, (C, C), 0)
    si = lax.broadcasted_iota(jnp.int32, (C, C), 1)
    tri = (si <= ti).astype(BF16)
    tref = (si < (ti // SUB) * SUB).astype(BF16)
    cum_lhs = jnp.concatenate([jnp.concatenate([tri] * 3, axis=1),
                               jnp.concatenate([tref] * 3, axis=1)], axis=0)
    rowi = lax.broadcasted_iota(jnp.int32, (C, 1), 0)
    rsub = rowi % SUB
    rblk = rowi // SUB
    ones_b = jnp.ones((HEAD_DIM, HEAD_DIM), BF16)
    nw = nw_ref[...]

    def body(c, carry):
        r0 = pl.multiple_of(c * C, C)
        rows = pl.ds(r0, C)
        q = _silu(q_ref[rows, :])
        f = lb + (1.0 - lb) * jax.nn.sigmoid(f_ref[rows, :])
        logf = jnp.log(f)
        k = 1.0 - f
        v = i_ref[rows, :]
        cum = _dot(cum_lhs, jnp.concatenate(_split3(logf), axis=0))
        gcs = cum[:C]
        gref = cum[C:]
        st = st_ref[...]
        o = _dot_nt((q * jnp.exp(gcs)).astype(BF16), st.astype(BF16))
        qt = q * jnp.exp(gcs - gref)
        lhs_parts, rhs_parts = [], []
        for blk in range(1, nsub):
            g_start = gcs[blk * SUB - 1:blk * SUB, :]
            kd = jnp.where(rowi < blk * SUB, k * jnp.exp(jnp.minimum(g_start - gcs, 0.0)), 0.0)
            lhs_parts.append(jnp.where(rblk == blk, qt, 0.0).astype(BF16))
            rhs_parts.append(kd.astype(BF16))
        att = _dot_nt(jnp.concatenate(lhs_parts, axis=1), jnp.concatenate(rhs_parts, axis=1))
        o = o + _dot(att.astype(BF16), v.astype(BF16))
        kp_ref[SUB:SUB + C, :] = k
        gp_ref[SUB:SUB + C, :] = gcs
        vp_ref[SUB:SUB + C, :] = v
        for d in range(SUB):
            sl = slice(SUB - d, SUB - d + C)
            pr = q * kp_ref[sl, :] * jnp.exp(gcs - gp_ref[sl, :])
            pr = jnp.where(rsub >= d, pr, 0.0)
            o = o + _dot(pr.astype(BF16), ones_b) * vp_ref[sl, :]
        g_last = gcs[C - 1:C, :]
        kdec = k * jnp.exp(g_last - gcs)
        st_ref[...] = jnp.exp(g_last) * st + _dot(v.T.astype(BF16), kdec.astype(BF16))
        on = o * lax.rsqrt(jnp.mean(o * o, axis=-1, keepdims=True) + EPS) * nw
        o_ref[rows, :] = (on * _silu(g_ref[rows, :])).astype(o_ref.dtype)
        return carry

    lax.fori_loop(0, nchunks, body, 0)


def _hgrn(proj, lb_logits, norm_w, *, bsz, rows_per_batch, layer):
    nh = HGRN_HEADS
    col = lambda part: (lambda b, h: (b, part * nh + h))
    blk = pl.BlockSpec((rows_per_batch, HEAD_DIM), col(0))
    return pl.pallas_call(
        functools.partial(_hgrn_kernel, nchunks=rows_per_batch // HGRN_CHUNK, layer=layer),
        grid=(bsz, nh),
        in_specs=[
            pl.BlockSpec((rows_per_batch, HEAD_DIM), col(0)),
            pl.BlockSpec((rows_per_batch, HEAD_DIM), col(1)),
            pl.BlockSpec((rows_per_batch, HEAD_DIM), col(2)),
            pl.BlockSpec((rows_per_batch, HEAD_DIM), col(3)),
            pl.BlockSpec((lb_logits.shape[0], HEAD_DIM), lambda b, h: (0, h)),
            pl.BlockSpec((1, HEAD_DIM), lambda b, h: (0, h)),
        ],
        out_specs=pl.BlockSpec((rows_per_batch, HEAD_DIM), lambda b, h: (b, h)),
        out_shape=jax.ShapeDtypeStruct((bsz * rows_per_batch, GROUP_WIDTH), BF16),
        scratch_shapes=[pltpu.VMEM((HEAD_DIM, HEAD_DIM), F32),
                        pltpu.VMEM((HGRN_CHUNK + HGRN_SUB, HEAD_DIM), F32),
                        pltpu.VMEM((HGRN_CHUNK + HGRN_SUB, HEAD_DIM), F32),
                        pltpu.VMEM((HGRN_CHUNK + HGRN_SUB, HEAD_DIM), F32)],
        compiler_params=_cparams(("arbitrary", "arbitrary")),
        name="hgrn2",
    )(proj, proj, proj, proj, lb_logits, norm_w.reshape(1, -1))


def _sb_kernel(q_ref, k_ref, v_ref, m2_ref, o_ref):
    qi = pl.program_id(2)
    B = CHUNK
    scale = HEAD_DIM ** -0.5
    q = q_ref[...].astype(BF16)
    tpos = qi * B + lax.broadcasted_iota(jnp.int32, (B, B), 0)
    lane = lax.broadcasted_iota(jnp.int32, (B, B), 1)
    m2 = m2_ref[...]

    def body(j, carry):
        o, later = carry
        kb = qi - j
        rows = pl.ds(pl.multiple_of(kb * B, B), B)
        logits = _dot_nt(q, k_ref[rows, :].astype(BF16)) * scale
        sp = jnp.log(1.0 + jnp.exp(-jnp.abs(logits)))
        log_beta = jnp.minimum(logits, 0.0) - sp
        log_rest = -jnp.maximum(logits, 0.0) - sp
        spos = kb * B + lane
        keep = (spos < tpos) & (spos >= PAD)
        lr = jnp.where(keep, log_rest, 0.0)
        hi = lr.astype(BF16)
        lo = (lr - hi.astype(F32)).astype(BF16)
        sums = _dot(jnp.concatenate([hi, lo], axis=1), m2)
        after = sums[:, :B]
        w = jnp.where(keep, jnp.exp(log_beta + after + later), 0.0)
        o = o + _dot(w.astype(BF16), v_ref[rows, :].astype(BF16))
        return o, later + sums[:, B:]

    zero = jnp.zeros((B, HEAD_DIM), F32)
    o, _ = lax.fori_loop(0, qi + 1, body, (zero, zero))
    o_ref[...] = o.astype(o_ref.dtype)


def _stick_breaking(proj, *, bsz, nblocks):
    nh = SB_HEADS
    rows = nblocks * CHUNK
    j = jnp.arange(CHUNK)
    later_mat = (j[:, None] > j[None, :]).astype(BF16)
    half = jnp.concatenate([later_mat, jnp.ones((CHUNK, CHUNK), BF16)], axis=1)
    m2 = jnp.concatenate([half, half], axis=0)
    return pl.pallas_call(
        _sb_kernel,
        grid=(bsz, nh, nblocks),
        in_specs=[
            pl.BlockSpec((CHUNK, HEAD_DIM), lambda b, h, i: (b * nblocks + i, 4 * nh + h)),
            pl.BlockSpec((rows, HEAD_DIM), lambda b, h, i: (b, 5 * nh + h)),
            pl.BlockSpec((rows, HEAD_DIM), lambda b, h, i: (b, 6 * nh + h)),
            pl.BlockSpec((2 * CHUNK, 2 * CHUNK), lambda b, h, i: (0, 0)),
        ],
        out_specs=pl.BlockSpec((CHUNK, HEAD_DIM), lambda b, h, i: (b * nblocks + i, h)),
        out_shape=jax.ShapeDtypeStruct((bsz * rows, GROUP_WIDTH), BF16),
        compiler_params=_cparams(("arbitrary", "arbitrary", "arbitrary")),
        name="stick_breaking",
    )(proj, proj, proj, m2)


def _mlp(h, norm_w, w1, w2, *, tm, name):
    u = _rmsnorm(h, norm_w, tm=256, out_dtype=BF16, name=name + "_norm")
    hid = _matmul(u, w1.astype(BF16), tm=tm, tn=1024, tk=D_MODEL, out_dtype=BF16, relu2=True, name=name + "_up")
    return _matmul(hid, w2.astype(BF16), tm=tm, tn=1024, tk=2048, out_dtype=F32, residual=h, name=name + "_down")


def kernel(x, meta_tokens, mix_norm_w, mlp_norm_w, final_norm_w, w_in_even, ssd_conv_w, ssd_conv_b, ssd_dt_bias, ssd_a_log, ssd_d, ssd_norm_w, conf_conv_w, conf_conv_b, conf_ln_w, conf_ln_b, w_out_even, w_in_odd, hgrn_lb_logits, hgrn_norm_w, w_out_odd, mlp_w1, mlp_w2):
    bsz, seq, d = x.shape
    lp = PAD + N_META + seq
    assert lp % CHUNK == 0 and d == D_MODEL
    nchunks = lp // CHUNK
    m = bsz * lp
    tm = 768 if m % 768 == 0 else CHUNK
    meta = jnp.broadcast_to(meta_tokens[None].astype(x.dtype), (bsz, N_META, d))
    h = jnp.concatenate([jnp.zeros((bsz, PAD, d), x.dtype), meta, x], axis=1).reshape(m, d)

    g0, g1, g2 = GROUP_WIDTH, GROUP_WIDTH + SSD_CONV_DIM, GROUP_WIDTH + SSD_CONV_DIM + SSD_HEADS
    w_in = w_in_even[0]
    w_main = jnp.concatenate([w_in[:, g0:g1], w_in[:, :g0], w_in[:, g2:]], axis=1).astype(BF16)
    w_dt = jnp.pad(w_in[:, g1:g2], ((0, 0), (0, LANES - SSD_HEADS))).astype(BF16)
    u = _rmsnorm(h, mix_norm_w[0], tm=256, out_dtype=BF16, name="l0_mix_norm")
    proj = _matmul(u, w_main, tm=tm, tn=1024, tk=d, out_dtype=F32, name="l0_in_proj")
    dt_raw = _matmul(u, w_dt, tm=tm, tn=LANES, tk=d, out_dtype=F32, name="l0_dt_proj")
    y_ssd = _ssd(proj, dt_raw, ssd_conv_w[0], ssd_conv_b[0], ssd_dt_bias[0], ssd_a_log[0], ssd_d[0],
                 ssd_norm_w[0], bsz=bsz, nchunks=nchunks)
    y_conf = _conformer(proj, conf_conv_w[0], conf_conv_b[0], conf_ln_w[0], conf_ln_b[0], bsz=bsz, nchunks=nchunks)
    mixed_in = jnp.concatenate([y_ssd, y_conf], axis=1)
    h = _matmul(mixed_in, w_out_even[0].astype(BF16), tm=tm, tn=1024, tk=2 * GROUP_WIDTH, out_dtype=F32,
                residual=h, name="l0_out_proj")
    h = _mlp(h, mlp_norm_w[0], mlp_w1[0], mlp_w2[0], tm=tm, name="l0_mlp")

    u = _rmsnorm(h, mix_norm_w[1], tm=256, out_dtype=BF16, name="l1_mix_norm")
    proj = _matmul(u, w_in_odd[0].astype(BF16), tm=tm, tn=1024, tk=d, out_dtype=F32, name="l1_in_proj")
    y_hgrn = _hgrn(proj, hgrn_lb_logits, hgrn_norm_w[0], bsz=bsz, rows_per_batch=lp, layer=1)
    y_sb = _stick_breaking(proj, bsz=bsz, nblocks=nchunks)
    mixed_in = jnp.concatenate([y_hgrn, y_sb], axis=1)
    h = _matmul(mixed_in, w_out_odd[0].astype(BF16), tm=tm, tn=1024, tk=2 * GROUP_WIDTH, out_dtype=F32,
                residual=h, name="l1_out_proj")
    h = _mlp(h, mlp_norm_w[1], mlp_w1[1], mlp_w2[1], tm=tm, name="l1_mlp")

    out = _rmsnorm(h, final_norm_w, tm=CHUNK, out_dtype=x.dtype, name="final_norm",
                   row_block_offset=1, out_rows=bsz * seq, blocks_per_batch=seq // CHUNK)
    return out.reshape(bsz, seq, d)
```

```python
import functools

import jax
import jax.numpy as jnp
from jax import lax
from jax.experimental import pallas as pl
from jax.experimental.pallas import tpu as pltpu

F32 = jnp.float32
BF16 = jnp.bfloat16

D_MODEL = 4096
N_META = 16
GROUP_WIDTH = 2048
CHUNK = 128
PAD = CHUNK - N_META
SSD_HEAD_DIM = 64
SSD_HEADS = 32
SSD_GROUPS = 8
SSD_STATE = 128
SSD_CONV = 4
SSD_GROUP_CH = GROUP_WIDTH // SSD_GROUPS
SSD_CONV_DIM = GROUP_WIDTH + 2 * SSD_GROUPS * SSD_STATE
CONF_WIDTH = 31
CONF_HALO = 32
HGRN_HEADS = 16
HGRN_CHUNK = 64
HGRN_SUB = 16
SB_HEADS = 16
SB_TILE = 3 * CHUNK
HEAD_DIM = 128
D_FF = 4 * D_MODEL
EPS = 1e-6
LANES = 128
VMEM_LIMIT = 56 * 1024 * 1024


def _cparams(sem):
    return pltpu.CompilerParams(dimension_semantics=sem, vmem_limit_bytes=VMEM_LIMIT)


def _silu(x):
    return x * jax.nn.sigmoid(x)


def _split3(x):
    hi = x.astype(BF16)
    r1 = x - hi.astype(F32)
    mid = r1.astype(BF16)
    lo = (r1 - mid.astype(F32)).astype(BF16)
    return hi, mid, lo


def _dot(a, b):
    return jnp.dot(a, b, preferred_element_type=F32)


def _dot_nt(a, b):
    return lax.dot_general(a, b, (((1,), (1,)), ((), ())), preferred_element_type=F32)


def _mm_kernel(a_ref, b_ref, *rest, nk, relu2, has_res):
    if has_res:
        r_ref, rest = rest[0], rest[1:]
    o_ref = rest[0]

    def finish(acc):
        if relu2:
            acc = jnp.square(jnp.maximum(acc, 0.0))
        if has_res:
            acc = acc + r_ref[...]
        o_ref[...] = acc.astype(o_ref.dtype)

    if nk == 1:
        finish(_dot(a_ref[...], b_ref[...]))
        return
    acc_ref = rest[1]
    k = pl.program_id(2)

    @pl.when(k == 0)
    def _():
        acc_ref[...] = jnp.zeros_like(acc_ref)

    acc_ref[...] += _dot(a_ref[...], b_ref[...])

    @pl.when(k == nk - 1)
    def _():
        finish(acc_ref[...])


def _matmul(a, b, *, tm, tn, tk, out_dtype, relu2=False, residual=None, name):
    m, kdim = a.shape
    n = b.shape[1]
    nk = kdim // tk
    assert m % tm == 0 and n % tn == 0 and kdim % tk == 0
    in_specs = [pl.BlockSpec((tm, tk), lambda i, j, k: (i, k)),
                pl.BlockSpec((tk, tn), lambda i, j, k: (k, j))]
    args = [a, b]
    if residual is not None:
        in_specs.append(pl.BlockSpec((tm, tn), lambda i, j, k: (i, j)))
        args.append(residual)
    scratch = [pltpu.VMEM((tm, tn), F32)] if nk > 1 else []
    return pl.pallas_call(
        functools.partial(_mm_kernel, nk=nk, relu2=relu2, has_res=residual is not None),
        grid=(m // tm, n // tn, nk),
        in_specs=in_specs,
        out_specs=pl.BlockSpec((tm, tn), lambda i, j, k: (i, j)),
        out_shape=jax.ShapeDtypeStruct((m, n), out_dtype),
        scratch_shapes=scratch,
        compiler_params=_cparams(("parallel", "parallel", "arbitrary")),
        name=name,
    )(*args)


def _rmsnorm_kernel(x_ref, w_ref, o_ref):
    x = x_ref[...]
    y = x * lax.rsqrt(jnp.mean(x * x, axis=-1, keepdims=True) + EPS)
    o_ref[...] = (y * w_ref[...]).astype(o_ref.dtype)


def _rmsnorm(x, w, *, tm, out_dtype, name, row_block_offset=0, out_rows=None, blocks_per_batch=None):
    m, d = x.shape
    if blocks_per_batch is None:
        nblk = m // tm
        in_map = lambda i: (i, 0)
        out_rows = m
    else:
        nblk = out_rows // tm
        in_per_batch = blocks_per_batch + row_block_offset
        in_map = lambda i: ((i // blocks_per_batch) * in_per_batch + row_block_offset + i % blocks_per_batch, 0)
    return pl.pallas_call(
        _rmsnorm_kernel,
        grid=(nblk,),
        in_specs=[pl.BlockSpec((tm, d), in_map), pl.BlockSpec((1, d), lambda i: (0, 0))],
        out_specs=pl.BlockSpec((tm, d), lambda i: (i, 0)),
        out_shape=jax.ShapeDtypeStruct((out_rows, d), out_dtype),
        compiler_params=_cparams(("parallel",)),
        name=name,
    )(x, w.reshape(1, d))


def _ssd_kernel(xbc_ref, z_ref, dt_ref, cw_ref, cb_ref, dtb_ref, ah_ref, e3_ref, dx_ref, nw_ref,
                o_ref, ext_ref, state_ref, y_ref):
    c = pl.program_id(1)
    C = CHUNK

    @pl.when(c == 0)
    def _():
        ext_ref[0:8, :] = jnp.zeros((8, SSD_CONV_DIM), F32)
        state_ref[...] = jnp.zeros_like(state_ref)

    ext_ref[8:8 + C, :] = xbc_ref[...]
    acc = cb_ref[...] + cw_ref[0:1, :] * ext_ref[5:5 + C, :]
    for k in range(1, SSD_CONV):
        acc = acc + cw_ref[k:k + 1, :] * ext_ref[5 + k:5 + k + C, :]
    ext_ref[0:8, :] = ext_ref[C:C + 8, :]
    row = c * C + lax.broadcasted_iota(jnp.int32, (C, 1), 0)
    valid = row >= PAD
    act = jnp.where(valid, _silu(acc), 0.0)

    dtv = dt_ref[...] + dtb_ref[...]
    dt = jnp.maximum(dtv, 0.0) + jnp.log(1.0 + jnp.exp(-jnp.abs(dtv)))
    dt = jnp.where(valid, dt, 0.0)
    a = dt * ah_ref[...]

    ti = lax.broadcasted_iota(jnp.int32, (C, C), 0)
    si = lax.broadcasted_iota(jnp.int32, (C, C), 1)
    tril = si <= ti
    tri_b = tril.astype(BF16)
    a_cs = sum(_dot(tri_b, p) for p in _split3(a))
    stacked = jnp.concatenate([dt, a_cs], axis=0)
    exp3 = _dot(jnp.concatenate(_split3(stacked), axis=1), e3_ref[...])
    dtx = exp3[:C]
    acsx = exp3[C:]
    a_last = acsx[C - 1:C, :]
    e_acs = jnp.exp(acsx)
    dec_st = jnp.exp(a_last - acsx)
    chunk_dec = jnp.exp(a_last)

    xs = act[:, :GROUP_WIDTH]
    xdt = xs * dtx
    xdec = (xdt * dec_st).astype(BF16)
    xdt_b = xdt.astype(BF16)
    a_cs_t = a_cs.T

    for g in range(SSD_GROUPS):
        cols = slice(g * SSD_GROUP_CH, (g + 1) * SSD_GROUP_CH)
        b_f = act[:, GROUP_WIDTH + g * SSD_STATE:GROUP_WIDTH + (g + 1) * SSD_STATE]
        c_f = act[:, GROUP_WIDTH + SSD_GROUPS * SSD_STATE + g * SSD_STATE:
                  GROUP_WIDTH + SSD_GROUPS * SSD_STATE + (g + 1) * SSD_STATE]
        b_b = b_f.astype(BF16)
        c_b = c_f.astype(BF16)
        cb = _dot_nt(c_b, b_b)
        s_g = state_ref[g]
        y_off = _dot(c_b, s_g.astype(BF16)) * e_acs[:, cols]
        for r in range(SSD_HEADS // SSD_GROUPS):
            h = g * (SSD_HEADS // SSD_GROUPS) + r
            seg = a_cs[:, h:h + 1] - a_cs_t[h:h + 1, :]
            lmat = jnp.exp(jnp.where(tril, seg, -jnp.inf))
            hc = slice(h * SSD_HEAD_DIM, (h + 1) * SSD_HEAD_DIM)
            y_ref[:, hc] = _dot((cb * lmat).astype(BF16), xdt_b[:, hc])
        state_ref[g] = s_g * chunk_dec[:, cols] + _dot(b_f.T.astype(BF16), xdec[:, cols])
        y = y_ref[:, cols] + y_off + dx_ref[:, cols] * xs[:, cols]
        y = y * _silu(z_ref[:, cols])
        y = y * lax.rsqrt(jnp.mean(y * y, axis=-1, keepdims=True) + EPS)
        o_ref[:, cols] = (y * nw_ref[:, cols]).astype(o_ref.dtype)


def _ssd(proj, dt_raw, conv_w, conv_b, dt_bias, a_log, d_skip, norm_w, *, bsz, nchunks):
    lane_pad = LANES - SSD_HEADS
    dtb = jnp.pad(dt_bias, (0, lane_pad)).reshape(1, LANES)
    a_head = jnp.pad(-jnp.exp(a_log), (0, lane_pad)).reshape(1, LANES)
    expand = (jnp.arange(LANES)[:, None] == (jnp.arange(GROUP_WIDTH) // SSD_HEAD_DIM)[None, :]).astype(BF16)
    e3 = jnp.concatenate([expand] * 3, axis=0)
    dx = jnp.repeat(d_skip, SSD_HEAD_DIM).reshape(1, GROUP_WIDTH)
    nb = nchunks
    full = lambda shape: pl.BlockSpec(shape, lambda b, c: (0,) * len(shape))
    return pl.pallas_call(
        _ssd_kernel,
        grid=(bsz, nchunks),
        in_specs=[
            pl.BlockSpec((CHUNK, SSD_CONV_DIM), lambda b, c: (b * nb + c, 0)),
            pl.BlockSpec((CHUNK, GROUP_WIDTH), lambda b, c: (b * nb + c, SSD_CONV_DIM // GROUP_WIDTH)),
            pl.BlockSpec((CHUNK, LANES), lambda b, c: (b * nb + c, 0)),
            full((SSD_CONV, SSD_CONV_DIM)), full((1, SSD_CONV_DIM)), full((1, LANES)), full((1, LANES)),
            full((3 * LANES, GROUP_WIDTH)), full((1, GROUP_WIDTH)), full((1, GROUP_WIDTH)),
        ],
        out_specs=pl.BlockSpec((CHUNK, GROUP_WIDTH), lambda b, c: (b * nb + c, 0)),
        out_shape=jax.ShapeDtypeStruct((bsz * nchunks * CHUNK, GROUP_WIDTH), BF16),
        scratch_shapes=[pltpu.VMEM((CHUNK + 8, SSD_CONV_DIM), F32),
                        pltpu.VMEM((SSD_GROUPS, SSD_STATE, SSD_GROUP_CH), F32),
                        pltpu.VMEM((CHUNK, GROUP_WIDTH), F32)],
        compiler_params=_cparams(("arbitrary", "arbitrary")),
        name="ssd",
    )(proj, proj, dt_raw, conv_w, conv_b.reshape(1, -1), dtb, a_head, e3, dx, norm_w.reshape(1, -1))


def _conf_kernel(a_ref, g_ref, cw_ref, cb_ref, lnw_ref, lnb_ref, o_ref, ext_ref, conv_ref):
    i = pl.program_id(1)
    R = CHUNK
    H = CONF_HALO

    @pl.when(i == 0)
    def _():
        ext_ref[0:H, :] = jnp.zeros((H, GROUP_WIDTH), F32)

    row = i * R + lax.broadcasted_iota(jnp.int32, (R, 1), 0)
    valid = row >= PAD
    ext_ref[H:H + R, :] = jnp.where(valid, a_ref[...] * jax.nn.sigmoid(g_ref[...]), 0.0)
    first = H - (CONF_WIDTH - 1)
    for ct in range(GROUP_WIDTH // LANES):
        cols = slice(ct * LANES, (ct + 1) * LANES)
        acc = cb_ref[:, cols] + cw_ref[0:1, cols] * ext_ref[first:first + R, cols]
        for k in range(1, CONF_WIDTH):
            acc = acc + cw_ref[k:k + 1, cols] * ext_ref[first + k:first + k + R, cols]
        conv_ref[:, cols] = acc
    ext_ref[0:H, :] = ext_ref[R:R + H, :]
    y = conv_ref[...]
    mu = jnp.mean(y, axis=-1, keepdims=True)
    yc = y - mu
    yn = yc * lax.rsqrt(jnp.mean(yc * yc, axis=-1, keepdims=True) + EPS)
    yn = yn * lnw_ref[...] + lnb_ref[...]
    o_ref[...] = jnp.where(valid, _silu(yn), 0.0).astype(o_ref.dtype)


def _conformer(proj, conv_w, conv_b, ln_w, ln_b, *, bsz, nchunks):
    nb = nchunks
    a_blk = SSD_CONV_DIM // GROUP_WIDTH + 1
    full = lambda shape: pl.BlockSpec(shape, lambda b, c: (0,) * len(shape))
    return pl.pallas_call(
        _conf_kernel,
        grid=(bsz, nchunks),
        in_specs=[
            pl.BlockSpec((CHUNK, GROUP_WIDTH), lambda b, c: (b * nb + c, a_blk)),
            pl.BlockSpec((CHUNK, GROUP_WIDTH), lambda b, c: (b * nb + c, a_blk + 1)),
            full((CONF_WIDTH, GROUP_WIDTH)), full((1, GROUP_WIDTH)), full((1, GROUP_WIDTH)), full((1, GROUP_WIDTH)),
        ],
        out_specs=pl.BlockSpec((CHUNK, GROUP_WIDTH), lambda b, c: (b * nb + c, 0)),
        out_shape=jax.ShapeDtypeStruct((bsz * nchunks * CHUNK, GROUP_WIDTH), BF16),
        scratch_shapes=[pltpu.VMEM((CHUNK + CONF_HALO, GROUP_WIDTH), F32),
                        pltpu.VMEM((CHUNK, GROUP_WIDTH), F32)],
        compiler_params=_cparams(("arbitrary", "arbitrary")),
        name="conformer",
    )(proj, proj, conv_w, conv_b.reshape(1, -1), ln_w.reshape(1, -1), ln_b.reshape(1, -1))


def _hgrn_kernel(q_ref, f_ref, i_ref, g_ref, lbl_ref, nw_ref, o_ref, st_ref, kp_ref, gp_ref, vp_ref,
                 *, nchunks, layer):
    C = HGRN_CHUNK
    SUB = HGRN_SUB
    nsub = C // SUB
    logits = lbl_ref[...]
    ex = jnp.exp(logits - jnp.max(logits, axis=0, keepdims=True))
    p = ex / jnp.sum(ex, axis=0, keepdims=True)
    lb = jnp.sum(p[:layer + 1], axis=0, keepdims=True) - p[0:1]

    st_ref[...] = jnp.zeros_like(st_ref)
    zpad = jnp.zeros((SUB, HEAD_DIM), F32)
    kp_ref[0:SUB, :] = zpad
    gp_ref[0:SUB, :] = zpad
    vp_ref[0:SUB, :] = zpad

    ti = lax.broadcasted_iota(jnp.int32, (C, C), 0)
    si = lax.broadcasted_iota(jnp.int32, (C, C), 1)
    tri = (si <= ti).astype(BF16)
    tref = (si < (ti // SUB) * SUB).astype(BF16)
    cum_lhs = jnp.concatenate([jnp.concatenate([tri] * 3, axis=1),
                               jnp.concatenate([tref] * 3, axis=1)], axis=0)
    rowi = lax.broadcasted_iota(jnp.int32, (C, 1), 0)
    rsub = rowi % SUB
    rblk = rowi // SUB
    ones_b = jnp.ones((HEAD_DIM, HEAD_DIM), BF16)
    nw = nw_ref[...]

    def body(c, carry):
        r0 = pl.multiple_of(c * C, C)
        rows = pl.ds(r0, C)
        q = _silu(q_ref[rows, :])
        f = lb + (1.0 - lb) * jax.nn.sigmoid(f_ref[rows, :])
        logf = jnp.log(f)
        k = 1.0 - f
        v = i_ref[rows, :]
        cum = _dot(cum_lhs, jnp.concatenate(_split3(logf), axis=0))
        gcs = cum[:C]
        gref = cum[C:]
        st = st_ref[...]
        o = _dot_nt((q * jnp.exp(gcs)).astype(BF16), st.astype(BF16))
        qt = q * jnp.exp(gcs - gref)
        lhs_parts, rhs_parts = [], []
        for blk in range(1, nsub):
            g_start = gcs[blk * SUB - 1:blk * SUB, :]
            kd = jnp.where(rowi < blk * SUB, k * jnp.exp(jnp.minimum(g_start - gcs, 0.0)), 0.0)
            lhs_parts.append(jnp.where(rblk == blk, qt, 0.0).astype(BF16))
            rhs_parts.append(kd.astype(BF16))
        att = _dot_nt(jnp.concatenate(lhs_parts, axis=1), jnp.concatenate(rhs_parts, axis=1))
        o = o + _dot(att.astype(BF16), v.astype(BF16))
        kp_ref[SUB:SUB + C, :] = k
        gp_ref[SUB:SUB + C, :] = gcs
        vp_ref[SUB:SUB + C, :] = v
        for d in range(SUB):
            sl = slice(SUB - d, SUB - d + C)
            pr = q * kp_ref[sl, :] * jnp.exp(gcs - gp_ref[sl, :])
            pr = jnp.where(rsub >= d, pr, 0.0)
            o = o + _dot(pr.astype(BF16), ones_b) * vp_ref[sl, :]
        g_last = gcs[C - 1:C, :]
        kdec = k * jnp.exp(g_last - gcs)
        st_ref[...] = jnp.exp(g_last) * st + _dot(v.T.astype(BF16), kdec.astype(BF16))
        on = o * lax.rsqrt(jnp.mean(o * o, axis=-1, keepdims=True) + EPS) * nw
        o_ref[rows, :] = (on * _silu(g_ref[rows, :])).astype(o_ref.dtype)
        return carry

    lax.fori_loop(0, nchunks, body, 0)


def _hgrn(proj, lb_logits, norm_w, *, bsz, rows_per_batch, layer):
    nh = HGRN_HEADS
    col = lambda part: (lambda b, h: (b, part * nh + h))
    blk = pl.BlockSpec((rows_per_batch, HEAD_DIM), col(0))
    return pl.pallas_call(
        functools.partial(_hgrn_kernel, nchunks=rows_per_batch // HGRN_CHUNK, layer=layer),
        grid=(bsz, nh),
        in_specs=[
            pl.BlockSpec((rows_per_batch, HEAD_DIM), col(0)),
            pl.BlockSpec((rows_per_batch, HEAD_DIM), col(1)),
            pl.BlockSpec((rows_per_batch, HEAD_DIM), col(2)),
            pl.BlockSpec((rows_per_batch, HEAD_DIM), col(3)),
            pl.BlockSpec((lb_logits.shape[0], HEAD_DIM), lambda b, h: (0, h)),
            pl.BlockSpec((1, HEAD_DIM), lambda b, h: (0, h)),
        ],
        out_specs=pl.BlockSpec((rows_per_batch, HEAD_DIM), lambda b, h: (b, h)),
        out_shape=jax.ShapeDtypeStruct((bsz * rows_per_batch, GROUP_WIDTH), BF16),
        scratch_shapes=[pltpu.VMEM((HEAD_DIM, HEAD_DIM), F32),
                        pltpu.VMEM((HGRN_CHUNK + HGRN_SUB, HEAD_DIM), F32),
                        pltpu.VMEM((HGRN_CHUNK + HGRN_SUB, HEAD_DIM), F32),
                        pltpu.VMEM((HGRN_CHUNK + HGRN_SUB, HEAD_DIM), F32)],
        compiler_params=_cparams(("arbitrary", "arbitrary")),
        name="hgrn2",
    )(proj, proj, proj, proj, lb_logits, norm_w.reshape(1, -1))


def _sb_kernel(q_ref, k_ref, v_ref, m2_ref, o_ref, acc_ref, later_ref):
    qi = pl.program_id(2)
    T = SB_TILE
    B = CHUNK
    nsub = T // B
    scale = HEAD_DIM ** -0.5
    q = q_ref[...].astype(BF16)
    m2 = m2_ref[...]
    acc_ref[...] = jnp.zeros_like(acc_ref)
    later_ref[...] = jnp.zeros_like(later_ref)

    def step(kc, masked):
        rows = pl.ds(pl.multiple_of(kc * T, B), T)
        logits = _dot_nt(q, k_ref[rows, :].astype(BF16)) * scale
        sp = jnp.log(1.0 + jnp.exp(-jnp.abs(logits)))
        log_beta = jnp.minimum(logits, 0.0) - sp
        lr = -jnp.maximum(logits, 0.0) - sp
        if masked:
            tpos = qi * T + lax.broadcasted_iota(jnp.int32, (T, T), 0)
            spos = kc * T + lax.broadcasted_iota(jnp.int32, (T, T), 1)
            keep = (spos < tpos) & (spos >= PAD)
            lr = jnp.where(keep, lr, 0.0)
        later = later_ref[...]
        ws = [None] * nsub
        for c in reversed(range(nsub)):
            cs = slice(c * B, (c + 1) * B)
            lrc = lr[:, cs]
            hi = lrc.astype(BF16)
            lo = (lrc - hi.astype(F32)).astype(BF16)
            sums = _dot(jnp.concatenate([hi, lo], axis=1), m2)
            w = jnp.exp(log_beta[:, cs] + sums[:, :B] + later)
            if masked:
                w = jnp.where(keep[:, cs], w, 0.0)
            ws[c] = w.astype(BF16)
            later = later + sums[:, B:]
        later_ref[...] = later
        acc_ref[...] += _dot(jnp.concatenate(ws, axis=1), v_ref[rows, :].astype(BF16))

    step(qi, True)

    def middle(j, carry):
        step(qi - j, False)
        return carry

    lax.fori_loop(1, qi, middle, 0)

    @pl.when(qi > 0)
    def _():
        step(0, True)

    o_ref[...] = acc_ref[...].astype(o_ref.dtype)


def _stick_breaking(proj, *, bsz, nblocks):
    nh = SB_HEADS
    rows = nblocks * CHUNK
    assert rows % SB_TILE == 0
    nq = rows // SB_TILE
    j = jnp.arange(CHUNK)
    later_mat = (j[:, None] > j[None, :]).astype(BF16)
    half = jnp.concatenate([later_mat, jnp.ones((CHUNK, CHUNK), BF16)], axis=1)
    m2 = jnp.concatenate([half, half], axis=0)
    return pl.pallas_call(
        _sb_kernel,
        grid=(bsz, nh, nq),
        in_specs=[
            pl.BlockSpec((SB_TILE, HEAD_DIM), lambda b, h, i: (b * nq + i, 4 * nh + h)),
            pl.BlockSpec((rows, HEAD_DIM), lambda b, h, i: (b, 5 * nh + h)),
            pl.BlockSpec((rows, HEAD_DIM), lambda b, h, i: (b, 6 * nh + h)),
            pl.BlockSpec((2 * CHUNK, 2 * CHUNK), lambda b, h, i: (0, 0)),
        ],
        out_specs=pl.BlockSpec((SB_TILE, HEAD_DIM), lambda b, h, i: (b * nq + i, h)),
        out_shape=jax.ShapeDtypeStruct((bsz * rows, GROUP_WIDTH), BF16),
        scratch_shapes=[pltpu.VMEM((SB_TILE, HEAD_DIM), F32), pltpu.VMEM((SB_TILE, CHUNK), F32)],
        compiler_params=_cparams(("arbitrary", "arbitrary", "arbitrary")),
        name="stick_breaking",
    )(proj, proj, proj, m2)


def _mlp(h, norm_w, w1, w2, *, tm, name):
    u = _rmsnorm(h, norm_w, tm=256, out_dtype=BF16, name=name + "_norm")
    hid = _matmul(u, w1.astype(BF16), tm=tm, tn=1024, tk=D_MODEL, out_dtype=BF16, relu2=True, name=name + "_up")
    return _matmul(hid, w2.astype(BF16), tm=tm, tn=1024, tk=2048, out_dtype=F32, residual=h, name=name + "_down")


def kernel(x, meta_tokens, mix_norm_w, mlp_norm_w, final_norm_w, w_in_even, ssd_conv_w, ssd_conv_b, ssd_dt_bias, ssd_a_log, ssd_d, ssd_norm_w, conf_conv_w, conf_conv_b, conf_ln_w, conf_ln_b, w_out_even, w_in_odd, hgrn_lb_logits, hgrn_norm_w, w_out_odd, mlp_w1, mlp_w2):
    bsz, seq, d = x.shape
    lp = PAD + N_META + seq
    assert lp % CHUNK == 0 and d == D_MODEL
    nchunks = lp // CHUNK
    m = bsz * lp
    tm = 768 if m % 768 == 0 else CHUNK
    meta = jnp.broadcast_to(meta_tokens[None].astype(x.dtype), (bsz, N_META, d))
    h = jnp.concatenate([jnp.zeros((bsz, PAD, d), x.dtype), meta, x], axis=1).reshape(m, d)

    g0, g1, g2 = GROUP_WIDTH, GROUP_WIDTH + SSD_CONV_DIM, GROUP_WIDTH + SSD_CONV_DIM + SSD_HEADS
    w_in = w_in_even[0]
    w_main = jnp.concatenate([w_in[:, g0:g1], w_in[:, :g0], w_in[:, g2:]], axis=1).astype(BF16)
    w_dt = jnp.pad(w_in[:, g1:g2], ((0, 0), (0, LANES - SSD_HEADS))).astype(BF16)
    u = _rmsnorm(h, mix_norm_w[0], tm=256, out_dtype=BF16, name="l0_mix_norm")
    proj = _matmul(u, w_main, tm=tm, tn=1024, tk=d, out_dtype=F32, name="l0_in_proj")
    dt_raw = _matmul(u, w_dt, tm=tm, tn=LANES, tk=d, out_dtype=F32, name="l0_dt_proj")
    y_ssd = _ssd(proj, dt_raw, ssd_conv_w[0], ssd_conv_b[0], ssd_dt_bias[0], ssd_a_log[0], ssd_d[0],
                 ssd_norm_w[0], bsz=bsz, nchunks=nchunks)
    y_conf = _conformer(proj, conf_conv_w[0], conf_conv_b[0], conf_ln_w[0], conf_ln_b[0], bsz=bsz, nchunks=nchunks)
    mixed_in = jnp.concatenate([y_ssd, y_conf], axis=1)
    h = _matmul(mixed_in, w_out_even[0].astype(BF16), tm=tm, tn=1024, tk=2 * GROUP_WIDTH, out_dtype=F32,
                residual=h, name="l0_out_proj")
    h = _mlp(h, mlp_norm_w[0], mlp_w1[0], mlp_w2[0], tm=tm, name="l0_mlp")

    u = _rmsnorm(h, mix_norm_w[1], tm=256, out_dtype=BF16, name="l1_mix_norm")
    proj = _matmul(u, w_in_odd[0].astype(BF16), tm=tm, tn=1024, tk=d, out_dtype=F32, name="l1_in_proj")
    y_hgrn = _hgrn(proj, hgrn_lb_logits, hgrn_norm_w[0], bsz=bsz, rows_per_batch=lp, layer=1)
    y_sb = _stick_breaking(proj, bsz=bsz, nblocks=nchunks)
    mixed_in = jnp.concatenate([y_hgrn, y_sb], axis=1)
    h = _matmul(mixed_in, w_out_odd[0].astype(BF16), tm=tm, tn=1024, tk=2 * GROUP_WIDTH, out_dtype=F32,
                residual=h, name="l1_out_proj")
    h = _mlp(h, mlp_norm_w[1], mlp_w1[1], mlp_w2[1], tm=tm, name="l1_mlp")

    out = _rmsnorm(h, final_norm_w, tm=CHUNK, out_dtype=x.dtype, name="final_norm",
                   row_block_offset=1, out_rows=bsz * seq, blocks_per_batch=seq // CHUNK)
    return out.reshape(bsz, seq, d)
```

```python
import functools

import jax
import jax.numpy as jnp
from jax import lax
from jax.experimental import pallas as pl
from jax.experimental.pallas import tpu as pltpu

F32 = jnp.float32
BF16 = jnp.bfloat16

D_MODEL = 4096
N_META = 16
GROUP_WIDTH = 2048
CHUNK = 128
PAD = CHUNK - N_META
SSD_HEAD_DIM = 64
SSD_HEADS = 32
SSD_GROUPS = 8
SSD_STATE = 128
SSD_CONV = 4
SSD_GROUP_CH = GROUP_WIDTH // SSD_GROUPS
SSD_CONV_DIM = GROUP_WIDTH + 2 * SSD_GROUPS * SSD_STATE
CONF_WIDTH = 31
CONF_HALO = 32
HGRN_HEADS = 16
HGRN_CHUNK = 64
HGRN_SUB = 16
HGRN_HEADS_PER_STEP = 2
SB_HEADS = 16
SB_TILE = 3 * CHUNK
HEAD_DIM = 128
D_FF = 4 * D_MODEL
EPS = 1e-6
LANES = 128
VMEM_LIMIT = 56 * 1024 * 1024


def _cparams(sem):
    return pltpu.CompilerParams(dimension_semantics=sem, vmem_limit_bytes=VMEM_LIMIT)


def _silu(x):
    return x * jax.nn.sigmoid(x)


def _split3(x):
    hi = x.astype(BF16)
    r1 = x - hi.astype(F32)
    mid = r1.astype(BF16)
    lo = (r1 - mid.astype(F32)).astype(BF16)
    return hi, mid, lo


def _dot(a, b):
    return jnp.dot(a, b, preferred_element_type=F32)


def _dot_nt(a, b):
    return lax.dot_general(a, b, (((1,), (1,)), ((), ())), preferred_element_type=F32)


def _mm_kernel(a_ref, b_ref, *rest, nk, relu2, has_res):
    if has_res:
        r_ref, rest = rest[0], rest[1:]
    o_ref = rest[0]

    def finish(acc):
        if relu2:
            acc = jnp.square(jnp.maximum(acc, 0.0))
        if has_res:
            acc = acc + r_ref[...]
        o_ref[...] = acc.astype(o_ref.dtype)

    if nk == 1:
        finish(_dot(a_ref[...], b_ref[...]))
        return
    acc_ref = rest[1]
    k = pl.program_id(2)

    @pl.when(k == 0)
    def _():
        acc_ref[...] = jnp.zeros_like(acc_ref)

    acc_ref[...] += _dot(a_ref[...], b_ref[...])

    @pl.when(k == nk - 1)
    def _():
        finish(acc_ref[...])


def _matmul(a, b, *, tm, tn, tk, out_dtype, relu2=False, residual=None, name):
    m, kdim = a.shape
    n = b.shape[1]
    nk = kdim // tk
    assert m % tm == 0 and n % tn == 0 and kdim % tk == 0
    in_specs = [pl.BlockSpec((tm, tk), lambda i, j, k: (i, k)),
                pl.BlockSpec((tk, tn), lambda i, j, k: (k, j))]
    args = [a, b]
    if residual is not None:
        in_specs.append(pl.BlockSpec((tm, tn), lambda i, j, k: (i, j)))
        args.append(residual)
    scratch = [pltpu.VMEM((tm, tn), F32)] if nk > 1 else []
    return pl.pallas_call(
        functools.partial(_mm_kernel, nk=nk, relu2=relu2, has_res=residual is not None),
        grid=(m // tm, n // tn, nk),
        in_specs=in_specs,
        out_specs=pl.BlockSpec((tm, tn), lambda i, j, k: (i, j)),
        out_shape=jax.ShapeDtypeStruct((m, n), out_dtype),
        scratch_shapes=scratch,
        compiler_params=_cparams(("parallel", "parallel", "arbitrary")),
        name=name,
    )(*args)


def _rmsnorm_kernel(x_ref, w_ref, o_ref):
    x = x_ref[...]
    y = x * lax.rsqrt(jnp.mean(x * x, axis=-1, keepdims=True) + EPS)
    o_ref[...] = (y * w_ref[...]).astype(o_ref.dtype)


def _rmsnorm(x, w, *, tm, out_dtype, name, row_block_offset=0, out_rows=None, blocks_per_batch=None):
    m, d = x.shape
    if blocks_per_batch is None:
        nblk = m // tm
        in_map = lambda i: (i, 0)
        out_rows = m
    else:
        nblk = out_rows // tm
        in_per_batch = blocks_per_batch + row_block_offset
        in_map = lambda i: ((i // blocks_per_batch) * in_per_batch + row_block_offset + i % blocks_per_batch, 0)
    return pl.pallas_call(
        _rmsnorm_kernel,
        grid=(nblk,),
        in_specs=[pl.BlockSpec((tm, d), in_map), pl.BlockSpec((1, d), lambda i: (0, 0))],
        out_specs=pl.BlockSpec((tm, d), lambda i: (i, 0)),
        out_shape=jax.ShapeDtypeStruct((out_rows, d), out_dtype),
        compiler_params=_cparams(("parallel",)),
        name=name,
    )(x, w.reshape(1, d))


def _ssd_kernel(xbc_ref, z_ref, dt_ref, cw_ref, cb_ref, dtb_ref, ah_ref, e3_ref, dx_ref, nw_ref,
                o_ref, ext_ref, state_ref, y_ref):
    c = pl.program_id(1)
    C = CHUNK

    @pl.when(c == 0)
    def _():
        ext_ref[0:8, :] = jnp.zeros((8, SSD_CONV_DIM), F32)
        state_ref[...] = jnp.zeros_like(state_ref)

    ext_ref[8:8 + C, :] = xbc_ref[...]
    acc = cb_ref[...] + cw_ref[0:1, :] * ext_ref[5:5 + C, :]
    for k in range(1, SSD_CONV):
        acc = acc + cw_ref[k:k + 1, :] * ext_ref[5 + k:5 + k + C, :]
    ext_ref[0:8, :] = ext_ref[C:C + 8, :]
    row = c * C + lax.broadcasted_iota(jnp.int32, (C, 1), 0)
    valid = row >= PAD
    act = jnp.where(valid, _silu(acc), 0.0)

    dtv = dt_ref[...] + dtb_ref[...]
    dt = jnp.maximum(dtv, 0.0) + jnp.log(1.0 + jnp.exp(-jnp.abs(dtv)))
    dt = jnp.where(valid, dt, 0.0)
    a = dt * ah_ref[...]

    ti = lax.broadcasted_iota(jnp.int32, (C, C), 0)
    si = lax.broadcasted_iota(jnp.int32, (C, C), 1)
    tril = si <= ti
    tri_b = tril.astype(BF16)
    a_cs = sum(_dot(tri_b, p) for p in _split3(a))
    stacked = jnp.concatenate([dt, a_cs], axis=0)
    exp3 = _dot(jnp.concatenate(_split3(stacked), axis=1), e3_ref[...])
    dtx = exp3[:C]
    acsx = exp3[C:]
    a_last = acsx[C - 1:C, :]
    e_acs = jnp.exp(acsx)
    dec_st = jnp.exp(a_last - acsx)
    chunk_dec = jnp.exp(a_last)

    xs = act[:, :GROUP_WIDTH]
    xdt = xs * dtx
    xdec = (xdt * dec_st).astype(BF16)
    xdt_b = xdt.astype(BF16)
    a_cs_t = a_cs.T

    for g in range(SSD_GROUPS):
        cols = slice(g * SSD_GROUP_CH, (g + 1) * SSD_GROUP_CH)
        b_f = act[:, GROUP_WIDTH + g * SSD_STATE:GROUP_WIDTH + (g + 1) * SSD_STATE]
        c_f = act[:, GROUP_WIDTH + SSD_GROUPS * SSD_STATE + g * SSD_STATE:
                  GROUP_WIDTH + SSD_GROUPS * SSD_STATE + (g + 1) * SSD_STATE]
        b_b = b_f.astype(BF16)
        c_b = c_f.astype(BF16)
        cb = _dot_nt(c_b, b_b)
        s_g = state_ref[g]
        y_off = _dot(c_b, s_g.astype(BF16)) * e_acs[:, cols]
        for r in range(SSD_HEADS // SSD_GROUPS):
            h = g * (SSD_HEADS // SSD_GROUPS) + r
            seg = a_cs[:, h:h + 1] - a_cs_t[h:h + 1, :]
            lmat = jnp.exp(jnp.where(tril, seg, -jnp.inf))
            hc = slice(h * SSD_HEAD_DIM, (h + 1) * SSD_HEAD_DIM)
            y_ref[:, hc] = _dot((cb * lmat).astype(BF16), xdt_b[:, hc])
        state_ref[g] = s_g * chunk_dec[:, cols] + _dot(b_f.T.astype(BF16), xdec[:, cols])
        y = y_ref[:, cols] + y_off + dx_ref[:, cols] * xs[:, cols]
        y = y * _silu(z_ref[:, cols])
        y = y * lax.rsqrt(jnp.mean(y * y, axis=-1, keepdims=True) + EPS)
        o_ref[:, cols] = (y * nw_ref[:, cols]).astype(o_ref.dtype)


def _ssd(proj, dt_raw, conv_w, conv_b, dt_bias, a_log, d_skip, norm_w, *, bsz, nchunks):
    lane_pad = LANES - SSD_HEADS
    dtb = jnp.pad(dt_bias, (0, lane_pad)).reshape(1, LANES)
    a_head = jnp.pad(-jnp.exp(a_log), (0, lane_pad)).reshape(1, LANES)
    expand = (jnp.arange(LANES)[:, None] == (jnp.arange(GROUP_WIDTH) // SSD_HEAD_DIM)[None, :]).astype(BF16)
    e3 = jnp.concatenate([expand] * 3, axis=0)
    dx = jnp.repeat(d_skip, SSD_HEAD_DIM).reshape(1, GROUP_WIDTH)
    nb = nchunks
    full = lambda shape: pl.BlockSpec(shape, lambda b, c: (0,) * len(shape))
    return pl.pallas_call(
        _ssd_kernel,
        grid=(bsz, nchunks),
        in_specs=[
            pl.BlockSpec((CHUNK, SSD_CONV_DIM), lambda b, c: (b * nb + c, 0)),
            pl.BlockSpec((CHUNK, GROUP_WIDTH), lambda b, c: (b * nb + c, SSD_CONV_DIM // GROUP_WIDTH)),
            pl.BlockSpec((CHUNK, LANES), lambda b, c: (b * nb + c, 0)),
            full((SSD_CONV, SSD_CONV_DIM)), full((1, SSD_CONV_DIM)), full((1, LANES)), full((1, LANES)),
            full((3 * LANES, GROUP_WIDTH)), full((1, GROUP_WIDTH)), full((1, GROUP_WIDTH)),
        ],
        out_specs=pl.BlockSpec((CHUNK, GROUP_WIDTH), lambda b, c: (b * nb + c, 0)),
        out_shape=jax.ShapeDtypeStruct((bsz * nchunks * CHUNK, GROUP_WIDTH), BF16),
        scratch_shapes=[pltpu.VMEM((CHUNK + 8, SSD_CONV_DIM), F32),
                        pltpu.VMEM((SSD_GROUPS, SSD_STATE, SSD_GROUP_CH), F32),
                        pltpu.VMEM((CHUNK, GROUP_WIDTH), F32)],
        compiler_params=_cparams(("arbitrary", "arbitrary")),
        name="ssd",
    )(proj, proj, dt_raw, conv_w, conv_b.reshape(1, -1), dtb, a_head, e3, dx, norm_w.reshape(1, -1))


def _conf_kernel(a_ref, g_ref, cw_ref, cb_ref, lnw_ref, lnb_ref, o_ref, ext_ref, conv_ref):
    i = pl.program_id(1)
    R = CHUNK
    H = CONF_HALO

    @pl.when(i == 0)
    def _():
        ext_ref[0:H, :] = jnp.zeros((H, GROUP_WIDTH), F32)

    row = i * R + lax.broadcasted_iota(jnp.int32, (R, 1), 0)
    valid = row >= PAD
    ext_ref[H:H + R, :] = jnp.where(valid, a_ref[...] * jax.nn.sigmoid(g_ref[...]), 0.0)
    first = H - (CONF_WIDTH - 1)
    for ct in range(GROUP_WIDTH // LANES):
        cols = slice(ct * LANES, (ct + 1) * LANES)
        acc = cb_ref[:, cols] + cw_ref[0:1, cols] * ext_ref[first:first + R, cols]
        for k in range(1, CONF_WIDTH):
            acc = acc + cw_ref[k:k + 1, cols] * ext_ref[first + k:first + k + R, cols]
        conv_ref[:, cols] = acc
    ext_ref[0:H, :] = ext_ref[R:R + H, :]
    y = conv_ref[...]
    mu = jnp.mean(y, axis=-1, keepdims=True)
    yc = y - mu
    yn = yc * lax.rsqrt(jnp.mean(yc * yc, axis=-1, keepdims=True) + EPS)
    yn = yn * lnw_ref[...] + lnb_ref[...]
    o_ref[...] = jnp.where(valid, _silu(yn), 0.0).astype(o_ref.dtype)


def _conformer(proj, conv_w, conv_b, ln_w, ln_b, *, bsz, nchunks):
    nb = nchunks
    a_blk = SSD_CONV_DIM // GROUP_WIDTH + 1
    full = lambda shape: pl.BlockSpec(shape, lambda b, c: (0,) * len(shape))
    return pl.pallas_call(
        _conf_kernel,
        grid=(bsz, nchunks),
        in_specs=[
            pl.BlockSpec((CHUNK, GROUP_WIDTH), lambda b, c: (b * nb + c, a_blk)),
            pl.BlockSpec((CHUNK, GROUP_WIDTH), lambda b, c: (b * nb + c, a_blk + 1)),
            full((CONF_WIDTH, GROUP_WIDTH)), full((1, GROUP_WIDTH)), full((1, GROUP_WIDTH)), full((1, GROUP_WIDTH)),
        ],
        out_specs=pl.BlockSpec((CHUNK, GROUP_WIDTH), lambda b, c: (b * nb + c, 0)),
        out_shape=jax.ShapeDtypeStruct((bsz * nchunks * CHUNK, GROUP_WIDTH), BF16),
        scratch_shapes=[pltpu.VMEM((CHUNK + CONF_HALO, GROUP_WIDTH), F32),
                        pltpu.VMEM((CHUNK, GROUP_WIDTH), F32)],
        compiler_params=_cparams(("arbitrary", "arbitrary")),
        name="conformer",
    )(proj, proj, conv_w, conv_b.reshape(1, -1), ln_w.reshape(1, -1), ln_b.reshape(1, -1))


def _hgrn_kernel(q_ref, f_ref, i_ref, g_ref, lbl_ref, nw_ref, o_ref, st_ref, kp_ref, gp_ref, vp_ref,
                 *, nchunks, layer):
    C = HGRN_CHUNK
    SUB = HGRN_SUB
    nsub = C // SUB
    logits = lbl_ref[...]
    ex = jnp.exp(logits - jnp.max(logits, axis=0, keepdims=True))
    p = ex / jnp.sum(ex, axis=0, keepdims=True)
    lb_all = jnp.sum(p[:layer + 1], axis=0, keepdims=True) - p[0:1]

    st_ref[...] = jnp.zeros_like(st_ref)
    zpad = jnp.zeros((HGRN_HEADS_PER_STEP, SUB, HEAD_DIM), F32)
    kp_ref[:, 0:SUB, :] = zpad
    gp_ref[:, 0:SUB, :] = zpad
    vp_ref[:, 0:SUB, :] = zpad

    ti = lax.broadcasted_iota(jnp.int32, (C, C), 0)
    si = lax.broadcasted_iota(jnp.int32, (C, C), 1)
    tri = (si <= ti).astype(BF16)
    tref = (si < (ti // SUB) * SUB).astype(BF16)
    cum_lhs = jnp.concatenate([jnp.concatenate([tri] * 3, axis=1),
                               jnp.concatenate([tref] * 3, axis=1)], axis=0)
    rowi = lax.broadcasted_iota(jnp.int32, (C, 1), 0)
    rsub = rowi % SUB
    rblk = rowi // SUB
    ones_b = jnp.ones((HEAD_DIM, HEAD_DIM), BF16)
    nw_all = nw_ref[...]

    def head_chunk(rows, hh):
        lanes = slice(hh * HEAD_DIM, (hh + 1) * HEAD_DIM)
        lb = lb_all[:, lanes]
        nw = nw_all[:, lanes]
        q = _silu(q_ref[rows, lanes])
        f = lb + (1.0 - lb) * jax.nn.sigmoid(f_ref[rows, lanes])
        logf = jnp.log(f)
        k = 1.0 - f
        v = i_ref[rows, lanes]
        cum = _dot(cum_lhs, jnp.concatenate(_split3(logf), axis=0))
        gcs = cum[:C]
        gref = cum[C:]
        st = st_ref[hh]
        o = _dot_nt((q * jnp.exp(gcs)).astype(BF16), st.astype(BF16))
        qt = q * jnp.exp(gcs - gref)
        lhs_parts, rhs_parts = [], []
        for blk in range(1, nsub):
            g_start = gcs[blk * SUB - 1:blk * SUB, :]
            kd = jnp.where(rowi < blk * SUB, k * jnp.exp(jnp.minimum(g_start - gcs, 0.0)), 0.0)
            lhs_parts.append(jnp.where(rblk == blk, qt, 0.0).astype(BF16))
            rhs_parts.append(kd.astype(BF16))
        att = _dot_nt(jnp.concatenate(lhs_parts, axis=1), jnp.concatenate(rhs_parts, axis=1))
        o = o + _dot(att.astype(BF16), v.astype(BF16))
        kp_ref[hh, SUB:SUB + C, :] = k
        gp_ref[hh, SUB:SUB + C, :] = gcs
        vp_ref[hh, SUB:SUB + C, :] = v
        for d in range(SUB):
            sl = slice(SUB - d, SUB - d + C)
            pr = q * kp_ref[hh, sl, :] * jnp.exp(gcs - gp_ref[hh, sl, :])
            pr = jnp.where(rsub >= d, pr, 0.0)
            o = o + _dot(pr.astype(BF16), ones_b) * vp_ref[hh, sl, :]
        g_last = gcs[C - 1:C, :]
        kdec = k * jnp.exp(g_last - gcs)
        st_ref[hh] = jnp.exp(g_last) * st + _dot(v.T.astype(BF16), kdec.astype(BF16))
        on = o * lax.rsqrt(jnp.mean(o * o, axis=-1, keepdims=True) + EPS) * nw
        o_ref[rows, lanes] = (on * _silu(g_ref[rows, lanes])).astype(o_ref.dtype)

    def body(c, carry):
        rows = pl.ds(pl.multiple_of(c * C, C), C)
        for hh in range(HGRN_HEADS_PER_STEP):
            head_chunk(rows, hh)
        return carry

    lax.fori_loop(0, nchunks, body, 0)


def _hgrn(proj, lb_logits, norm_w, *, bsz, rows_per_batch, layer):
    hps = HGRN_HEADS_PER_STEP
    nsteps = HGRN_HEADS // hps
    width = hps * HEAD_DIM
    col = lambda part: (lambda b, h: (b, part * nsteps + h))
    pad_rows = HGRN_CHUNK + HGRN_SUB
    return pl.pallas_call(
        functools.partial(_hgrn_kernel, nchunks=rows_per_batch // HGRN_CHUNK, layer=layer),
        grid=(bsz, nsteps),
        in_specs=[
            pl.BlockSpec((rows_per_batch, width), col(0)),
            pl.BlockSpec((rows_per_batch, width), col(1)),
            pl.BlockSpec((rows_per_batch, width), col(2)),
            pl.BlockSpec((rows_per_batch, width), col(3)),
            pl.BlockSpec((lb_logits.shape[0], width), lambda b, h: (0, h)),
            pl.BlockSpec((1, width), lambda b, h: (0, h)),
        ],
        out_specs=pl.BlockSpec((rows_per_batch, width), lambda b, h: (b, h)),
        out_shape=jax.ShapeDtypeStruct((bsz * rows_per_batch, GROUP_WIDTH), BF16),
        scratch_shapes=[pltpu.VMEM((hps, HEAD_DIM, HEAD_DIM), F32),
                        pltpu.VMEM((hps, pad_rows, HEAD_DIM), F32),
                        pltpu.VMEM((hps, pad_rows, HEAD_DIM), F32),
                        pltpu.VMEM((hps, pad_rows, HEAD_DIM), F32)],
        compiler_params=_cparams(("arbitrary", "arbitrary")),
        name="hgrn2",
    )(proj, proj, proj, proj, lb_logits, norm_w.reshape(1, -1))


def _sb_kernel(q_ref, k_ref, v_ref, m2_ref, o_ref, acc_ref, later_ref):
    qi = pl.program_id(2)
    T = SB_TILE
    B = CHUNK
    nsub = T // B
    scale = HEAD_DIM ** -0.5
    q = q_ref[...].astype(BF16)
    m2 = m2_ref[...]
    acc_ref[...] = jnp.zeros_like(acc_ref)
    later_ref[...] = jnp.zeros_like(later_ref)

    def step(kc, masked):
        rows = pl.ds(pl.multiple_of(kc * T, B), T)
        logits = _dot_nt(q, k_ref[rows, :].astype(BF16)) * scale
        sp = jnp.log(1.0 + jnp.exp(-jnp.abs(logits)))
        log_beta = jnp.minimum(logits, 0.0) - sp
        lr = -jnp.maximum(logits, 0.0) - sp
        if masked:
            tpos = qi * T + lax.broadcasted_iota(jnp.int32, (T, T), 0)
            spos = kc * T + lax.broadcasted_iota(jnp.int32, (T, T), 1)
            keep = (spos < tpos) & (spos >= PAD)
            lr = jnp.where(keep, lr, 0.0)
        later = later_ref[...]
        ws = [None] * nsub
        for c in reversed(range(nsub)):
            cs = slice(c * B, (c + 1) * B)
            lrc = lr[:, cs]
            hi = lrc.astype(BF16)
            lo = (lrc - hi.astype(F32)).astype(BF16)
            sums = _dot(jnp.concatenate([hi, lo], axis=1), m2)
            w = jnp.exp(log_beta[:, cs] + sums[:, :B] + later)
            if masked:
                w = jnp.where(keep[:, cs], w, 0.0)
            ws[c] = w.astype(BF16)
            later = later + sums[:, B:]
        later_ref[...] = later
        acc_ref[...] += _dot(jnp.concatenate(ws, axis=1), v_ref[rows, :].astype(BF16))

    step(qi, True)

    def middle(j, carry):
        step(qi - j, False)
        return carry

    lax.fori_loop(1, qi, middle, 0)

    @pl.when(qi > 0)
    def _():
        step(0, True)

    o_ref[...] = acc_ref[...].astype(o_ref.dtype)


def _stick_breaking(proj, *, bsz, nblocks):
    nh = SB_HEADS
    rows = nblocks * CHUNK
    assert rows % SB_TILE == 0
    nq = rows // SB_TILE
    j = jnp.arange(CHUNK)
    later_mat = (j[:, None] > j[None, :]).astype(BF16)
    half = jnp.concatenate([later_mat, jnp.ones((CHUNK, CHUNK), BF16)], axis=1)
    m2 = jnp.concatenate([half, half], axis=0)
    return pl.pallas_call(
        _sb_kernel,
        grid=(bsz, nh, nq),
        in_specs=[
            pl.BlockSpec((SB_TILE, HEAD_DIM), lambda b, h, i: (b * nq + i, 4 * nh + h)),
            pl.BlockSpec((rows, HEAD_DIM), lambda b, h, i: (b, 5 * nh + h)),
            pl.BlockSpec((rows, HEAD_DIM), lambda b, h, i: (b, 6 * nh + h)),
            pl.BlockSpec((2 * CHUNK, 2 * CHUNK), lambda b, h, i: (0, 0)),
        ],
        out_specs=pl.BlockSpec((SB_TILE, HEAD_DIM), lambda b, h, i: (b * nq + i, h)),
        out_shape=jax.ShapeDtypeStruct((bsz * rows, GROUP_WIDTH), BF16),
        scratch_shapes=[pltpu.VMEM((SB_TILE, HEAD_DIM), F32), pltpu.VMEM((SB_TILE, CHUNK), F32)],
        compiler_params=_cparams(("arbitrary", "arbitrary", "arbitrary")),
        name="stick_breaking",
    )(proj, proj, proj, m2)


def _mlp(h, norm_w, w1, w2, *, tm, name):
    u = _rmsnorm(h, norm_w, tm=256, out_dtype=BF16, name=name + "_norm")
    hid = _matmul(u, w1.astype(BF16), tm=tm, tn=1024, tk=D_MODEL, out_dtype=BF16, relu2=True, name=name + "_up")
    return _matmul(hid, w2.astype(BF16), tm=tm, tn=1024, tk=2048, out_dtype=F32, residual=h, name=name + "_down")


def kernel(x, meta_tokens, mix_norm_w, mlp_norm_w, final_norm_w, w_in_even, ssd_conv_w, ssd_conv_b, ssd_dt_bias, ssd_a_log, ssd_d, ssd_norm_w, conf_conv_w, conf_conv_b, conf_ln_w, conf_ln_b, w_out_even, w_in_odd, hgrn_lb_logits, hgrn_norm_w, w_out_odd, mlp_w1, mlp_w2):
    bsz, seq, d = x.shape
    lp = PAD + N_META + seq
    assert lp % CHUNK == 0 and d == D_MODEL
    nchunks = lp // CHUNK
    m = bsz * lp
    tm = 768 if m % 768 == 0 else CHUNK
    meta = jnp.broadcast_to(meta_tokens[None].astype(x.dtype), (bsz, N_META, d))
    h = jnp.concatenate([jnp.zeros((bsz, PAD, d), x.dtype), meta, x], axis=1).reshape(m, d)

    g0, g1, g2 = GROUP_WIDTH, GROUP_WIDTH + SSD_CONV_DIM, GROUP_WIDTH + SSD_CONV_DIM + SSD_HEADS
    w_in = w_in_even[0]
    w_main = jnp.concatenate([w_in[:, g0:g1], w_in[:, :g0], w_in[:, g2:]], axis=1).astype(BF16)
    w_dt = jnp.pad(w_in[:, g1:g2], ((0, 0), (0, LANES - SSD_HEADS))).astype(BF16)
    u = _rmsnorm(h, mix_norm_w[0], tm=256, out_dtype=BF16, name="l0_mix_norm")
    proj = _matmul(u, w_main, tm=tm, tn=1024, tk=d, out_dtype=F32, name="l0_in_proj")
    dt_raw = _matmul(u, w_dt, tm=tm, tn=LANES, tk=d, out_dtype=F32, name="l0_dt_proj")
    y_ssd = _ssd(proj, dt_raw, ssd_conv_w[0], ssd_conv_b[0], ssd_dt_bias[0], ssd_a_log[0], ssd_d[0],
                 ssd_norm_w[0], bsz=bsz, nchunks=nchunks)
    y_conf = _conformer(proj, conf_conv_w[0], conf_conv_b[0], conf_ln_w[0], conf_ln_b[0], bsz=bsz, nchunks=nchunks)
    mixed_in = jnp.concatenate([y_ssd, y_conf], axis=1)
    h = _matmul(mixed_in, w_out_even[0].astype(BF16), tm=tm, tn=1024, tk=2 * GROUP_WIDTH, out_dtype=F32,
                residual=h, name="l0_out_proj")
    h = _mlp(h, mlp_norm_w[0], mlp_w1[0], mlp_w2[0], tm=tm, name="l0_mlp")

    u = _rmsnorm(h, mix_norm_w[1], tm=256, out_dtype=BF16, name="l1_mix_norm")
    proj = _matmul(u, w_in_odd[0].astype(BF16), tm=tm, tn=1024, tk=d, out_dtype=F32, name="l1_in_proj")
    y_hgrn = _hgrn(proj, hgrn_lb_logits, hgrn_norm_w[0], bsz=bsz, rows_per_batch=lp, layer=1)
    y_sb = _stick_breaking(proj, bsz=bsz, nblocks=nchunks)
    mixed_in = jnp.concatenate([y_hgrn, y_sb], axis=1)
    h = _matmul(mixed_in, w_out_odd[0].astype(BF16), tm=tm, tn=1024, tk=2 * GROUP_WIDTH, out_dtype=F32,
                residual=h, name="l1_out_proj")
    h = _mlp(h, mlp_norm_w[1], mlp_w1[1], mlp_w2[1], tm=tm, name="l1_mlp")

    out = _rmsnorm(h, final_norm_w, tm=CHUNK, out_dtype=x.dtype, name="final_norm",
                   row_block_offset=1, out_rows=bsz * seq, blocks_per_batch=seq // CHUNK)
    return out.reshape(bsz, seq, d)
```
